```python
import jax, jax.numpy as jnp
from jax import lax
import numpy as np

D_MODEL = 4096
BATCH = 1
SEQ = 16384
DEPTH = 1

MIX_WIDTH = D_MODEL
POOL_WIDTH = D_MODEL // 4
POOL_WINDOWS = (2, 4, 8, 16)
POOL_GROUPS = len(POOL_WINDOWS)
POOL_GROUP_DIM = POOL_WIDTH // POOL_GROUPS
HEAD_DIM = 128
N_HEADS = (MIX_WIDTH - POOL_WIDTH) // HEAD_DIM
N_KV = 4
HPG = N_HEADS // N_KV
KV_WIDTH = N_KV * HEAD_DIM
ROT_DIM = HEAD_DIM // 4
ROPE_THETA = 500000.0
CMP_LEN = 32
CMP_STRIDE = 16
CMP_HIDDEN = 256
SLC_LEN = 64
SLC_TOPK = 16
WIN = 512
Q_BLOCK = 128
D_FF = 11008
CONV_W = 3
PLE_DIM = 256
EPS = 1e-6
NEG = -1e30
BIG = 1e9
OFF_Q = POOL_WIDTH
OFF_KV = OFF_Q + N_HEADS * HEAD_DIM
OFF_G = OFF_KV + 6 * KV_WIDTH
IN_WIDTH = OFF_G + 3 * N_HEADS

kernel_name = "hymba_pool_nsa_convffn_ple"


def rms_norm(x, w):
    xf = x.astype(jnp.float32)
    y = xf * lax.rsqrt(jnp.mean(xf * xf, axis=-1, keepdims=True) + EPS)
    return (y * w.astype(jnp.float32)).astype(x.dtype)


def rope_partial(x, pos):
    half = ROT_DIM // 2
    inv = ROPE_THETA ** (-jnp.arange(0, ROT_DIM, 2, dtype=jnp.float32) / ROT_DIM)
    ang = pos.astype(jnp.float32)[..., None] * inv
    cos = jnp.cos(ang)[:, :, None, :].astype(x.dtype)
    sin = jnp.sin(ang)[:, :, None, :].astype(x.dtype)
    x1 = x[..., :half]
    x2 = x[..., half:ROT_DIM]
    return jnp.concatenate([x1 * cos - x2 * sin, x2 * cos + x1 * sin, x[..., ROT_DIM:]], axis=-1)


def masked_softmax(s, mask):
    s = jnp.where(mask, s.astype(jnp.float32), NEG)
    return jax.nn.softmax(s, axis=-1) * mask.astype(jnp.float32)


def pool_mixer(u, w_pool, scale):
    B, S, _ = u.shape
    ug = u.reshape(B, S, POOL_GROUPS, POOL_GROUP_DIM).astype(jnp.float32)
    t1 = jnp.arange(1, S + 1, dtype=jnp.float32)
    outs = []
    for gi, w in enumerate(POOL_WINDOWS):
        ch = ug[:, :, gi]
        c = jnp.cumsum(ch, axis=1)
        prev = jnp.pad(c, ((0, 0), (w, 0), (0, 0)))[:, :S]
        cnt = jnp.minimum(t1, float(w))[None, :, None]
        outs.append((c - prev) / cnt - ch)
    m = jnp.stack(outs, axis=2).astype(u.dtype)
    y = jnp.einsum("bsgc,gcd->bsgd", m, w_pool).reshape(B, S, POOL_WIDTH)
    return y * scale


def compress_blocks(kv, pos_emb, w1, w2):
    B, S = kv.shape[:2]
    n_cmp = (S - CMP_LEN) // CMP_STRIDE + 1
    idx = jnp.arange(n_cmp)[:, None] * CMP_STRIDE + jnp.arange(CMP_LEN)[None, :]
    blocks = kv[:, idx] + pos_emb[None, None, :, None, :]
    blocks = blocks.transpose(0, 1, 3, 2, 4).reshape(B, n_cmp, N_KV, CMP_LEN * HEAD_DIM)
    return jax.nn.gelu(blocks @ w1) @ w2


def nsa_attention(q, k_cmp, v_cmp, k_slc, v_slc, k_win, v_win, gates):
    B, S = q.shape[:2]
    n_cmp = k_cmp.shape[1]
    n_slc = S // SLC_LEN
    topk = min(SLC_TOPK, n_slc)
    r = SLC_LEN // CMP_STRIDE
    c = CMP_LEN // CMP_STRIDE
    scale = HEAD_DIM ** -0.5
    cmp_end = jnp.arange(n_cmp) * CMP_STRIDE + CMP_LEN - 1
    kb = k_slc.reshape(B, n_slc, SLC_LEN, N_KV, HEAD_DIM).transpose(0, 3, 1, 2, 4)
    vb = v_slc.reshape(B, n_slc, SLC_LEN, N_KV, HEAD_DIM).transpose(0, 3, 1, 2, 4)
    kw = jnp.pad(k_win, ((0, 0), (WIN, 0), (0, 0), (0, 0)))
    vw = jnp.pad(v_win, ((0, 0), (WIN, 0), (0, 0), (0, 0)))
    bi = jnp.arange(B)[:, None, None, None]
    gi = jnp.arange(N_KV)[None, :, None, None]
    blk_ids = jnp.arange(n_slc)
    pad_imp = r * n_slc + 4 - n_cmp

    def block(qb):
        s0 = qb * Q_BLOCK
        t = s0 + jnp.arange(Q_BLOCK)
        qx = lax.dynamic_slice_in_dim(q, s0, Q_BLOCK, axis=1).reshape(B, Q_BLOCK, N_KV, HPG, HEAD_DIM)
        s_c = jnp.einsum("bqghd,bngd->bghqn", qx, k_cmp) * scale
        p_c = masked_softmax(s_c, cmp_end[None, :] <= t[:, None])
        o_c = jnp.einsum("bghqn,bngd->bqghd", p_c.astype(v_cmp.dtype), v_cmp)
        imp = jnp.pad(p_c.sum(axis=2), ((0, 0), (0, 0), (0, 0), (0, pad_imp)))
        imp_slc = jnp.zeros(imp.shape[:3] + (n_slc,), jnp.float32)
        for m in range(r):
            for n in range(c):
                o = m + n
                imp_slc = imp_slc + imp[..., o:o + r * n_slc:r]
        cur = (t // SLC_LEN)[:, None]
        jj = blk_ids[None, :]
        forced = (jj == 0) | (jj == cur) | (jj == cur - 1)
        sel = jnp.where(jj > cur, NEG, jnp.where(forced, BIG, imp_slc))
        _, idx = lax.top_k(sel, topk)
        ks = kb[bi, gi, idx].reshape(B, N_KV, Q_BLOCK, topk * SLC_LEN, HEAD_DIM)
        vs = vb[bi, gi, idx].reshape(B, N_KV, Q_BLOCK, topk * SLC_LEN, HEAD_DIM)
        kpos = (idx[..., None] * SLC_LEN + jnp.arange(SLC_LEN)).reshape(B, N_KV, 1, Q_BLOCK, topk * SLC_LEN)
        s_s = jnp.einsum("bqghd,bgqkd->bghqk", qx, ks) * scale
        p_s = masked_softmax(s_s, kpos <= t[None, None, None, :, None])
        o_s = jnp.einsum("bghqk,bgqkd->bqghd", p_s.astype(vs.dtype), vs)
        kband = lax.dynamic_slice_in_dim(kw, s0, WIN + Q_BLOCK, axis=1)
        vband = lax.dynamic_slice_in_dim(vw, s0, WIN + Q_BLOCK, axis=1)
        kp = s0 - WIN + jnp.arange(WIN + Q_BLOCK)
        m_w = (kp[None, :] <= t[:, None]) & (kp[None, :] > t[:, None] - WIN) & (kp[None, :] >= 0)
        s_w = jnp.einsum("bqghd,bkgd->bghqk", qx, kband) * scale
        p_w = masked_softmax(s_w, m_w)
        o_w = jnp.einsum("bghqk,bkgd->bqghd", p_w.astype(vband.dtype), vband)
        g = lax.dynamic_slice_in_dim(gates, s0, Q_BLOCK, axis=1)
        shp = (B, Q_BLOCK, N_HEADS, HEAD_DIM)
        out = (g[..., 0:1] * o_c.reshape(shp) + g[..., 1:2] * o_s.reshape(shp)
               + g[..., 2:3] * o_w.reshape(shp))
        return out.reshape(B, Q_BLOCK, N_HEADS * HEAD_DIM)

    out = lax.map(block, jnp.arange(S // Q_BLOCK))
    return out.transpose(1, 0, 2, 3).reshape(B, S, N_HEADS * HEAD_DIM)


def causal_dwconv(a, w, b):
    S = a.shape[1]
    ap = jnp.pad(a, ((0, 0), (CONV_W - 1, 0), (0, 0)))
    y = b
    for k in range(CONV_W):
        y = y + ap[:, k:k + S] * w[k]
    return y


def setup_inputs(seed: int = 0) -> dict:
    key = jax.random.key(seed)
    ks = jax.random.split(key, 32)

    def nrm(k, shape, s):
        return jax.random.normal(k, shape, jnp.float32) * s

    def gain(k, shape):
        return 1.0 + 0.02 * jax.random.normal(k, shape, jnp.float32)

    L = DEPTH
    return {
        "x": nrm(ks[0], (BATCH, SEQ, D_MODEL), 1.0),
        "p": nrm(ks[1], (DEPTH, BATCH, SEQ, PLE_DIM), 1.0),
        "positions": jnp.broadcast_to(jnp.arange(SEQ, dtype=jnp.int32)[None, :], (BATCH, SEQ)),
        "norm1_w": gain(ks[2], (L, D_MODEL)),
        "w_in": nrm(ks[3], (L, D_MODEL, IN_WIDTH), D_MODEL ** -0.5),
        "w_pool": nrm(ks[4], (L, POOL_GROUPS, POOL_GROUP_DIM, POOL_GROUP_DIM), POOL_GROUP_DIM ** -0.5),
        "pool_scale": gain(ks[5], (L, POOL_WIDTH)),
        "q_norm_w": gain(ks[6], (L, HEAD_DIM)),
        "k_norm_cmp_w": gain(ks[7], (L, HEAD_DIM)),
        "k_norm_slc_w": gain(ks[8], (L, HEAD_DIM)),
        "k_norm_win_w": gain(ks[9], (L, HEAD_DIM)),
        "cmp_pos_k": nrm(ks[10], (L, CMP_LEN, HEAD_DIM), 0.02),
        "cmp_pos_v": nrm(ks[11], (L, CMP_LEN, HEAD_DIM), 0.02),
        "cmp_k_w1": nrm(ks[12], (L, CMP_LEN * HEAD_DIM, CMP_HIDDEN), (CMP_LEN * HEAD_DIM) ** -0.5),
        "cmp_k_w2": nrm(ks[13], (L, CMP_HIDDEN, HEAD_DIM), CMP_HIDDEN ** -0.5),
        "cmp_v_w1": nrm(ks[14], (L, CMP_LEN * HEAD_DIM, CMP_HIDDEN), (CMP_LEN * HEAD_DIM) ** -0.5),
        "cmp_v_w2": nrm(ks[15], (L, CMP_HIDDEN, HEAD_DIM), CMP_HIDDEN ** -0.5),
        "w_o": nrm(ks[16], (L, MIX_WIDTH, D_MODEL), MIX_WIDTH ** -0.5),
        "norm2_w": gain(ks[17], (L, D_MODEL)),
        "w_ffn_in": nrm(ks[18], (L, D_MODEL, 2 * D_FF), D_MODEL ** -0.5),
        "conv_w": nrm(ks[19], (L, CONV_W, D_FF), CONV_W ** -0.5),
        "conv_b": nrm(ks[20], (L, D_FF), 0.01),
        "w_ffn_out": nrm(ks[21], (L, D_FF, D_MODEL), D_FF ** -0.5),
        "w_ple_proj": nrm(ks[22], (L, PLE_DIM, D_MODEL), PLE_DIM ** -0.5),
        "ple_norm_w": gain(ks[23], (L, D_MODEL)),
        "ple_gate_norm_w": gain(ks[24], (L, D_MODEL)),
        "w_ple_gate": nrm(ks[25], (L, D_MODEL, D_MODEL), D_MODEL ** -0.5),
    }


def reference(x, p, positions, norm1_w, w_in, w_pool, pool_scale, q_norm_w, k_norm_cmp_w, k_norm_slc_w,
              k_norm_win_w, cmp_pos_k, cmp_pos_v, cmp_k_w1, cmp_k_w2, cmp_v_w1, cmp_v_w2, w_o, norm2_w,
              w_ffn_in, conv_w, conv_b, w_ffn_out, w_ple_proj, ple_norm_w, ple_gate_norm_w, w_ple_gate):
    B, S, _ = x.shape
    n_cmp = (S - CMP_LEN) // CMP_STRIDE + 1
    cmp_end = jnp.arange(n_cmp) * CMP_STRIDE + CMP_LEN - 1
    pos_cmp = positions[:, cmp_end]
    h = x
    for i in range(DEPTH):
        xn = rms_norm(h, norm1_w[i])
        z = xn @ w_in[i]

        def kv(j):
            return z[..., OFF_KV + j * KV_WIDTH:OFF_KV + (j + 1) * KV_WIDTH].reshape(B, S, N_KV, HEAD_DIM)

        u = z[..., :POOL_WIDTH]
        q = z[..., OFF_Q:OFF_KV].reshape(B, S, N_HEADS, HEAD_DIM)
        q = rope_partial(rms_norm(q, q_norm_w[i]), positions)
        kc = compress_blocks(kv(0), cmp_pos_k[i], cmp_k_w1[i], cmp_k_w2[i])
        kc = rope_partial(rms_norm(kc, k_norm_cmp_w[i]), pos_cmp)
        vc = compress_blocks(kv(1), cmp_pos_v[i], cmp_v_w1[i], cmp_v_w2[i])
        k_s = rope_partial(rms_norm(kv(2), k_norm_slc_w[i]), positions)
        v_s = kv(3)
        k_w = rope_partial(rms_norm(kv(4), k_norm_win_w[i]), positions)
        v_w = kv(5)
        gates = jax.nn.sigmoid(z[..., OFF_G:].astype(jnp.float32)).astype(x.dtype).reshape(B, S, N_HEADS, 3)
        o_attn = nsa_attention(q, kc, vc, k_s, v_s, k_w, v_w, gates)
        o_pool = pool_mixer(u, w_pool[i], pool_scale[i])
        h = h + jnp.concatenate([o_pool, o_attn], axis=-1) @ w_o[i]
        hn = rms_norm(h, norm2_w[i])
        ab = hn @ w_ffn_in[i]
        a = causal_dwconv(ab[..., :D_FF], conv_w[i], conv_b[i])
        h = h + (jax.nn.silu(a) * ab[..., D_FF:]) @ w_ffn_out[i]
        e = rms_norm(p[i] @ w_ple_proj[i], ple_norm_w[i])
        g = jax.nn.sigmoid(rms_norm(h, ple_gate_norm_w[i]) @ w_ple_gate[i])
        h = h + e * g
    return h
```

```python
import functools

import jax
import jax.numpy as jnp
from jax import lax
from jax.experimental import pallas as pl
from jax.experimental.pallas import tpu as pltpu

D_MODEL = 4096
POOL_WIDTH = 1024
POOL_WINDOWS = (2, 4, 8, 16)
POOL_GROUP_DIM = 256
POOL_HALO = 16
HEAD_DIM = 128
N_HEADS = 24
N_KV = 4
HPG = N_HEADS // N_KV
GW = HPG * HEAD_DIM
ROT_DIM = 32
ROT_HALF = ROT_DIM // 2
ROPE_THETA = 500000.0
CMP_LEN = 32
CMP_STRIDE = 16
CMP_HIDDEN = 256
SLC_LEN = 64
SLC_TOPK = 16
WIN = 512
Q_BLOCK = 128
D_FF = 11008
PLE_DIM = 256
EPS = 1e-6
NEG = -1e30
BIG = 1e9
M_FLOOR = -1e20
PICKED = -3e38
QKV_W = N_HEADS * HEAD_DIM
GATE_PAD = 128
N_GATE = 3 * N_HEADS
SLC_CHUNK = 512
WIN_SPAN = WIN + Q_BLOCK
VMEM_LIMIT = 56 * 1024 * 1024

_ARB = pltpu.ARBITRARY


def _cparams(n_axes, vmem=VMEM_LIMIT):
    return pltpu.CompilerParams(dimension_semantics=(_ARB,) * n_axes, vmem_limit_bytes=vmem)


def _dot(a, b):
    return jnp.dot(a, b, preferred_element_type=jnp.float32)


def _sigmoid(x):
    return 1.0 / (1.0 + jnp.exp(-x))


def _rmsnorm_body(x_ref, w_ref, o_ref):
    x = x_ref[...]
    ms = jnp.mean(x * x, axis=-1, keepdims=True)
    o_ref[...] = (x * lax.rsqrt(ms + EPS) * w_ref[...]).astype(o_ref.dtype)


def _rmsnorm(x, w, tm=256):
    s, d = x.shape
    return pl.pallas_call(
        _rmsnorm_body,
        out_shape=jax.ShapeDtypeStruct((s, d), jnp.bfloat16),
        grid=(s // tm,),
        in_specs=[pl.BlockSpec((tm, d), lambda i: (i, 0)), pl.BlockSpec((1, d), lambda i: (0, 0))],
        out_specs=pl.BlockSpec((tm, d), lambda i: (i, 0)),
        compiler_params=_cparams(1),
        name="rmsnorm",
    )(x, w.reshape(1, d))


def _matmul_body(a_ref, w_ref, o_ref):
    o_ref[...] = _dot(a_ref[...], w_ref[...]).astype(o_ref.dtype)


def _matmul(a, w, tm, tn, name):
    m, k = a.shape
    n = w.shape[1]
    return pl.pallas_call(
        _matmul_body,
        out_shape=jax.ShapeDtypeStruct((m, n), jnp.float32),
        grid=(m // tm, n // tn),
        in_specs=[pl.BlockSpec((tm, k), lambda i, j: (i, 0)), pl.BlockSpec((k, tn), lambda i, j: (0, j))],
        out_specs=pl.BlockSpec((tm, tn), lambda i, j: (i, j)),
        compiler_params=_cparams(2),
        name=name,
    )(a, w)


def _prep_body(zq_ref, zkv_ref, zg_ref, posr_ref, posc_ref, invc_ref, invr_ref, qw_ref, ksw_ref, kww_ref,
               qT_ref, ks_ref, kw_ref, vsT_ref, vwT_ref, kvc_ref, gT_ref):
    ang_t = invc_ref[...] * posr_ref[...].astype(jnp.float32)
    cos_t, sin_t = jnp.cos(ang_t), jnp.sin(ang_t)
    ang_n = posc_ref[...].astype(jnp.float32) * invr_ref[...]
    cos_n, sin_n = jnp.cos(ang_n), jnp.sin(ang_n)
    lane = lax.broadcasted_iota(jnp.int32, (Q_BLOCK, HEAD_DIM), 1)
    sin_n = jnp.where(lane < ROT_HALF, -sin_n, sin_n)
    qw = qw_ref[...]
    scale = HEAD_DIM ** -0.5

    for h in range(N_HEADS):
        xt = zq_ref[:, h * HEAD_DIM:(h + 1) * HEAD_DIM].T
        ms = jnp.mean(xt * xt, axis=0, keepdims=True)
        y = xt * lax.rsqrt(ms + EPS) * qw
        x1, x2 = y[0:ROT_HALF], y[ROT_HALF:ROT_DIM]
        yr = jnp.concatenate([x1 * cos_t - x2 * sin_t, x2 * cos_t + x1 * sin_t, y[ROT_DIM:]], axis=0)
        hh = h % HPG
        qT_ref[h // HPG, 0, :, hh * HEAD_DIM:(hh + 1) * HEAD_DIM] = (yr * scale).astype(qT_ref.dtype)

    def k_norm_rope(x, w):
        ms = jnp.mean(x * x, axis=-1, keepdims=True)
        y = x * lax.rsqrt(ms + EPS) * w
        partner = jnp.where(lane < ROT_HALF, pltpu.roll(y, HEAD_DIM - ROT_HALF, 1), pltpu.roll(y, ROT_HALF, 1))
        return y * cos_n + partner * sin_n

    for g in range(N_KV):
        def col(j):
            return zkv_ref[:, (j * N_KV + g) * HEAD_DIM:(j * N_KV + g + 1) * HEAD_DIM]
        kvc_ref[0, g] = col(0)
        kvc_ref[1, g] = col(1)
        ks_ref[g] = k_norm_rope(col(2), ksw_ref[...]).astype(ks_ref.dtype)
        vsT_ref[g] = col(3).T.astype(vsT_ref.dtype)
        kw_ref[g] = k_norm_rope(col(4), kww_ref[...]).astype(kw_ref.dtype)
        vwT_ref[g] = col(5).T.astype(vwT_ref.dtype)

    gT_ref[0] = _sigmoid(zg_ref[...]).T[0:N_GATE, :]


def _prep(z_main, z_g, positions, inv, q_norm_w, k_norm_slc_w, k_norm_win_w):
    s = z_main.shape[0]
    nqb = s // Q_BLOCK
    inv_col = inv.reshape(ROT_HALF, 1)
    inv_row = jnp.zeros((1, HEAD_DIM), jnp.float32).at[0, :ROT_DIM].set(jnp.concatenate([inv, inv]))
    bf = jnp.bfloat16
    out_shape = (
        jax.ShapeDtypeStruct((N_KV, nqb, HEAD_DIM, GW), bf),
        jax.ShapeDtypeStruct((N_KV, s, HEAD_DIM), bf),
        jax.ShapeDtypeStruct((N_KV, s, HEAD_DIM), bf),
        jax.ShapeDtypeStruct((N_KV, HEAD_DIM, s), bf),
        jax.ShapeDtypeStruct((N_KV, HEAD_DIM, s), bf),
        jax.ShapeDtypeStruct((2, N_KV, s, HEAD_DIM), jnp.float32),
        jax.ShapeDtypeStruct((nqb, N_GATE, Q_BLOCK), jnp.float32),
    )
    const = lambda shape: pl.BlockSpec(shape, lambda i: (0,) * len(shape))
    in_specs = [
        pl.BlockSpec((Q_BLOCK, QKV_W), lambda i: (i, 0)),
        pl.BlockSpec((Q_BLOCK, QKV_W), lambda i: (i, 1)),
        pl.BlockSpec((Q_BLOCK, GATE_PAD), lambda i: (i, 0)),
        pl.BlockSpec((1, Q_BLOCK), lambda i: (0, i)),
        pl.BlockSpec((Q_BLOCK, 1), lambda i: (i, 0)),
        const((ROT_HALF, 1)), const((1, HEAD_DIM)), const((HEAD_DIM, 1)), const((1, HEAD_DIM)), const((1, HEAD_DIM)),
    ]
    out_specs = (
        pl.BlockSpec((N_KV, 1, HEAD_DIM, GW), lambda i: (0, i, 0, 0)),
        pl.BlockSpec((N_KV, Q_BLOCK, HEAD_DIM), lambda i: (0, i, 0)),
        pl.BlockSpec((N_KV, Q_BLOCK, HEAD_DIM), lambda i: (0, i, 0)),
        pl.BlockSpec((N_KV, HEAD_DIM, Q_BLOCK), lambda i: (0, 0, i)),
        pl.BlockSpec((N_KV, HEAD_DIM, Q_BLOCK), lambda i: (0, 0, i)),
        pl.BlockSpec((2, N_KV, Q_BLOCK, HEAD_DIM), lambda i: (0, 0, i, 0)),
        pl.BlockSpec((1, N_GATE, Q_BLOCK), lambda i: (i, 0, 0)),
    )
    return pl.pallas_call(
        _prep_body, out_shape=out_shape, grid=(nqb,), in_specs=in_specs, out_specs=out_specs,
        compiler_params=_cparams(1), name="head_prep",
    )(z_main, z_main, z_g, positions.reshape(1, s), positions.reshape(s, 1), inv_col, inv_row,
      q_norm_w.reshape(HEAD_DIM, 1), k_norm_slc_w.reshape(1, HEAD_DIM), k_norm_win_w.reshape(1, HEAD_DIM))


def _compress_body(r_ref, pe_ref, w1_ref, w2_ref, kw_ref, posc_ref, invr_ref, kc_ref, vcT_ref):
    j = pl.program_id(1)
    r = r_ref[0, 0]
    nc = r.shape[0]
    half = CMP_STRIDE * HEAD_DIM
    pe = pe_ref[0]
    w1 = w1_ref[0]
    first = _dot((r + pe[0:1]).astype(jnp.bfloat16), w1[0:half])
    second = _dot((r + pe[1:2]).astype(jnp.bfloat16), w1[half:2 * half])
    row = lax.broadcasted_iota(jnp.int32, (nc, 1), 0)
    second_next = jnp.where(row == nc - 1, 0.0, pltpu.roll(second, nc - 1, 0))
    hid = jax.nn.gelu(first + second_next)
    out = _dot(hid.astype(jnp.bfloat16), w2_ref[0])

    @pl.when(j == 0)
    def _():
        ms = jnp.mean(out * out, axis=-1, keepdims=True)
        y = out * lax.rsqrt(ms + EPS) * kw_ref[...]
        ang = posc_ref[...].astype(jnp.float32) * invr_ref[...]
        lane = lax.broadcasted_iota(jnp.int32, y.shape, 1)
        sin = jnp.sin(ang)
        sin = jnp.where(lane < ROT_HALF, -sin, sin)
        partner = jnp.where(lane < ROT_HALF, pltpu.roll(y, HEAD_DIM - ROT_HALF, 1), pltpu.roll(y, ROT_HALF, 1))
        kc_ref[0] = (y * jnp.cos(ang) + partner * sin).astype(kc_ref.dtype)

    @pl.when(j == 1)
    def _():
        vcT_ref[0] = out.T.astype(vcT_ref.dtype)


def _compress(kvc, pe, w1, w2, k_norm_cmp_w, pos_cmp, inv):
    s = kvc.shape[2]
    nc = s // CMP_STRIDE
    half = CMP_STRIDE * HEAD_DIM
    r = kvc.reshape(2, N_KV, nc, half)
    inv_row = jnp.zeros((1, HEAD_DIM), jnp.float32).at[0, :ROT_DIM].set(jnp.concatenate([inv, inv]))
    return pl.pallas_call(
        _compress_body,
        out_shape=(jax.ShapeDtypeStruct((N_KV, nc, HEAD_DIM), jnp.bfloat16),
                   jax.ShapeDtypeStruct((N_KV, HEAD_DIM, nc), jnp.bfloat16)),
        grid=(N_KV, 2),
        in_specs=[
            pl.BlockSpec((1, 1, nc, half), lambda g, j: (j, g, 0, 0)),
            pl.BlockSpec((1, 2, half), lambda g, j: (j, 0, 0)),
            pl.BlockSpec((1, 2 * half, CMP_HIDDEN), lambda g, j: (j, 0, 0)),
            pl.BlockSpec((1, CMP_HIDDEN, HEAD_DIM), lambda g, j: (j, 0, 0)),
            pl.BlockSpec((1, HEAD_DIM), lambda g, j: (0, 0)),
            pl.BlockSpec((nc, 1), lambda g, j: (0, 0)),
            pl.BlockSpec((1, HEAD_DIM), lambda g, j: (0, 0)),
        ],
        out_specs=(pl.BlockSpec((1, nc, HEAD_DIM), lambda g, j: (g, 0, 0)),
                   pl.BlockSpec((1, HEAD_DIM, nc), lambda g, j: (g, 0, 0))),
        compiler_params=_cparams(2), name="compress",
    )(r, pe, w1, w2, k_norm_cmp_w.reshape(1, HEAD_DIM), pos_cmp, inv_row)


def _softmax_cols(s_masked):
    m = jnp.maximum(jnp.max(s_masked, axis=0, keepdims=True), M_FLOOR)
    e = jnp.exp(s_masked - m)
    l = jnp.sum(e, axis=0, keepdims=True)
    return e * (1.0 / jnp.where(l == 0.0, 1.0, l))


def _attn_body(qT_ref, kc_ref, vcT_ref, ks_ref, vsT_ref, kw_ref, vwT_ref, g_ref, o_ref,
               imp_ref, sel_ref, m_ref, l_ref, acc_ref):
    g = pl.program_id(0)
    qb = pl.program_id(1)
    s0 = qb * Q_BLOCK
    nc = kc_ref.shape[1]
    ns = sel_ref.shape[0]
    bf = jnp.bfloat16
    qT = qT_ref[0, 0]
    tq = s0 + (lax.broadcasted_iota(jnp.int32, (1, GW), 1) & (Q_BLOCK - 1))
    tq1 = s0 + lax.broadcasted_iota(jnp.int32, (1, Q_BLOCK), 1)

    sc = _dot(kc_ref[0], qT)
    cmp_end = lax.broadcasted_iota(jnp.int32, (nc, 1), 0) * CMP_STRIDE + (CMP_LEN - 1)
    pc = _softmax_cols(jnp.where(cmp_end <= tq, sc, NEG))
    o_c = _dot(vcT_ref[0], pc.astype(bf))

    imp = pc[:, 0:Q_BLOCK]
    for hh in range(1, HPG):
        imp = imp + pc[:, hh * Q_BLOCK:(hh + 1) * Q_BLOCK]
    imp_ref[0:nc, :] = imp
    imp_ref[nc:nc + 8, :] = jnp.zeros((8, Q_BLOCK), jnp.float32)
    ratio = SLC_LEN // CMP_STRIDE
    tap = lambda o: imp_ref[pl.ds(o, ns, stride=ratio), :]
    imp_slc = tap(0) + tap(4) + 2.0 * (tap(1) + tap(2) + tap(3))

    blk = lax.broadcasted_iota(jnp.int32, (ns, 1), 0)
    cur = tq1 >> 6
    forced = (blk == 0) | (blk == cur) | (blk == cur - 1)
    sel = jnp.where(blk > cur, NEG, jnp.where(forced, BIG, imp_slc))
    chosen = jnp.zeros((ns, Q_BLOCK), jnp.float32)
    for _ in range(SLC_TOPK):
        mx = jnp.max(sel, axis=0, keepdims=True)
        first = jnp.min(jnp.where(sel == mx, blk, ns), axis=0, keepdims=True)
        pick = blk == first
        chosen = jnp.where(pick, 1.0, chosen)
        sel = jnp.where(pick, PICKED, sel)
    sel_ref[...] = jnp.where(chosen > 0.0, 0.0, NEG)

    m_ref[...] = jnp.full(m_ref.shape, M_FLOOR, jnp.float32)
    l_ref[...] = jnp.zeros(l_ref.shape, jnp.float32)
    acc_ref[...] = jnp.zeros(acc_ref.shape, jnp.float32)
    blocks_per_chunk = SLC_CHUNK // SLC_LEN
    key_in_chunk = lax.broadcasted_iota(jnp.int32, (SLC_CHUNK, 1), 0)

    def slc_step(c, carry):
        k0 = pl.multiple_of(c * SLC_CHUNK, SLC_CHUNK)
        s = _dot(ks_ref[0, pl.ds(k0, SLC_CHUNK), :], qT)
        rows = [jnp.broadcast_to(sel_ref[pl.ds(c * blocks_per_chunk + b, 1), :], (SLC_LEN, Q_BLOCK))
                for b in range(blocks_per_chunk)]
        bias = jnp.where(k0 + key_in_chunk <= tq1, jnp.concatenate(rows, axis=0), NEG)
        sm = s + jnp.concatenate([bias] * HPG, axis=1)
        m_old = m_ref[...]
        m_new = jnp.maximum(m_old, jnp.max(sm, axis=0, keepdims=True))
        alpha = jnp.exp(m_old - m_new)
        p = jnp.exp(sm - m_new)
        l_ref[...] = alpha * l_ref[...] + jnp.sum(p, axis=0, keepdims=True)
        acc_ref[...] = alpha * acc_ref[...] + _dot(vsT_ref[0, :, pl.ds(k0, SLC_CHUNK)], p.astype(bf))
        m_ref[...] = m_new
        return carry

    lax.fori_loop(0, (s0 + Q_BLOCK + SLC_CHUNK - 1) // SLC_CHUNK, slc_step, 0)
    l_s = l_ref[...]
    o_s = acc_ref[...] * (1.0 / jnp.where(l_s == 0.0, 1.0, l_s))

    start = pl.multiple_of(jnp.maximum(s0 - WIN, 0), Q_BLOCK)
    sw = _dot(kw_ref[0, pl.ds(start, WIN_SPAN), :], qT)
    kp = start + lax.broadcasted_iota(jnp.int32, (WIN_SPAN, 1), 0)
    pw = _softmax_cols(jnp.where((kp <= tq) & (kp > tq - WIN), sw, NEG))
    o_w = _dot(vwT_ref[0, :, pl.ds(start, WIN_SPAN)], pw.astype(bf))

    for hh in range(HPG):
        base = (g * HPG + hh) * 3
        sl = slice(hh * Q_BLOCK, (hh + 1) * Q_BLOCK)
        oh = (g_ref[0, pl.ds(base, 1), :] * o_c[:, sl] + g_ref[0, pl.ds(base + 1, 1), :] * o_s[:, sl]
              + g_ref[0, pl.ds(base + 2, 1), :] * o_w[:, sl])
        o_ref[:, sl] = oh.T.astype(o_ref.dtype)


def _attention(qT, kc, vcT, ks, vsT, kw, vwT, gT):
    nqb = qT.shape[1]
    s = ks.shape[1]
    nc = kc.shape[1]
    ns = s // SLC_LEN
    whole = lambda shape: pl.BlockSpec(shape, lambda g, i: (g, 0, 0))
    return pl.pallas_call(
        _attn_body,
        out_shape=jax.ShapeDtypeStruct((s, QKV_W), jnp.bfloat16),
        grid=(N_KV, nqb),
        in_specs=[
            pl.BlockSpec((1, 1, HEAD_DIM, GW), lambda g, i: (g, i, 0, 0)),
            whole((1, nc, HEAD_DIM)), whole((1, HEAD_DIM, nc)),
            whole((1, s, HEAD_DIM)), whole((1, HEAD_DIM, s)),
            whole((1, s, HEAD_DIM)), whole((1, HEAD_DIM, s)),
            pl.BlockSpec((1, N_GATE, Q_BLOCK), lambda g, i: (i, 0, 0)),
        ],
        out_specs=pl.BlockSpec((Q_BLOCK, GW), lambda g, i: (i, g)),
        scratch_shapes=[
            pltpu.VMEM((nc + 8, Q_BLOCK), jnp.float32),
            pltpu.VMEM((ns, Q_BLOCK), jnp.float32),
            pltpu.VMEM((1, GW), jnp.float32),
            pltpu.VMEM((1, GW), jnp.float32),
            pltpu.VMEM((HEAD_DIM, GW), jnp.float32),
        ],
        compiler_params=_cparams(2), name="nsa_attention",
    )(qT, kc, vcT, ks, vsT, kw, vwT, gT)


def _pool_body(u_ref, w_ref, sc_ref, o_ref, ext_ref):
    i = pl.program_id(0)
    tm = u_ref.shape[0]

    @pl.when(i == 0)
    def _():
        ext_ref[0:POOL_HALO, :] = jnp.zeros((POOL_HALO, POOL_WIDTH), jnp.float32)

    ext_ref[POOL_HALO:POOL_HALO + tm, :] = u_ref[...]
    t1 = (i * tm + 1 + lax.broadcasted_iota(jnp.int32, (tm, 1), 0)).astype(jnp.float32)
    for gi, w in enumerate(POOL_WINDOWS):
        cols = slice(gi * POOL_GROUP_DIM, (gi + 1) * POOL_GROUP_DIM)
        u = ext_ref[POOL_HALO:POOL_HALO + tm, cols]
        ws = u
        for k in range(1, w):
            ws = ws + ext_ref[POOL_HALO - k:POOL_HALO - k + tm, cols]
        m = ws / jnp.minimum(t1, float(w)) - u
        y = _dot(m.astype(jnp.bfloat16), w_ref[gi])
        o_ref[:, cols] = (y * sc_ref[:, cols]).astype(o_ref.dtype)
    ext_ref[0:POOL_HALO, :] = ext_ref[tm:tm + POOL_HALO, :]


def _pool(z_main, w_pool, pool_scale, tm=512):
    s = z_main.shape[0]
    u_blk = (2 * QKV_W) // POOL_WIDTH
    return pl.pallas_call(
        _pool_body,
        out_shape=jax.ShapeDtypeStruct((s, POOL_WIDTH), jnp.bfloat16),
        grid=(s // tm,),
        in_specs=[
            pl.BlockSpec((tm, POOL_WIDTH), lambda i: (i, u_blk)),
            pl.BlockSpec((len(POOL_WINDOWS), POOL_GROUP_DIM, POOL_GROUP_DIM), lambda i: (0, 0, 0)),
            pl.BlockSpec((1, POOL_WIDTH), lambda i: (0, 0)),
        ],
        out_specs=pl.BlockSpec((tm, POOL_WIDTH), lambda i: (i, 0)),
        scratch_shapes=[pltpu.VMEM((tm + POOL_HALO, POOL_WIDTH), jnp.float32)],
        compiler_params=_cparams(1), name="pool_mixer",
    )(z_main, w_pool, pool_scale.reshape(1, POOL_WIDTH))


def _oproj_body(x_ref, op_ref, oa_ref, wp_ref, wa_ref, o_ref):
    o_ref[...] = x_ref[...] + _dot(op_ref[...], wp_ref[...]) + _dot(oa_ref[...], wa_ref[...])


def _oproj(x, o_pool, o_attn, w_pool_rows, w_attn_rows, tm=1024, tn=512):
    s, d = x.shape
    return pl.pallas_call(
        _oproj_body,
        out_shape=jax.ShapeDtypeStruct((s, d), jnp.float32),
        grid=(s // tm, d // tn),
        in_specs=[
            pl.BlockSpec((tm, tn), lambda i, j: (i, j)),
            pl.BlockSpec((tm, POOL_WIDTH), lambda i, j: (i, 0)),
            pl.BlockSpec((tm, QKV_W), lambda i, j: (i, 0)),
            pl.BlockSpec((POOL_WIDTH, tn), lambda i, j: (0, j)),
            pl.BlockSpec((QKV_W, tn), lambda i, j: (0, j)),
        ],
        out_specs=pl.BlockSpec((tm, tn), lambda i, j: (i, j)),
        compiler_params=_cparams(2), name="out_proj",
    )(x, o_pool, o_attn, w_pool_rows, w_attn_rows)


def _ffn_in_body(hn_ref, wg_ref, wu_ref, cw_ref, cb_ref, o_ref, carry_ref):
    i = pl.program_id(0)
    j = pl.program_id(1)
    tm, tn = o_ref.shape
    hn = hn_ref[...]
    a = _dot(hn, wg_ref[...])
    b = _dot(hn, wu_ref[...])
    col = pl.multiple_of(j * tn, tn)
    prev = jnp.where(i == 0, 0.0, carry_ref[:, pl.ds(col, tn)])
    carry_ref[:, pl.ds(col, tn)] = a[tm - 8:tm, :]
    row = lax.broadcasted_iota(jnp.int32, (tm, 1), 0)
    a1 = jnp.where(row == 0, prev[7:8], pltpu.roll(a, 1, 0))
    a2 = jnp.where(row == 0, prev[6:7], jnp.where(row == 1, prev[7:8], pltpu.roll(a, 2, 0)))
    y = cb_ref[...] + a2 * cw_ref[0:1, :] + a1 * cw_ref[1:2, :] + a * cw_ref[2:3, :]
    o_ref[...] = (y * _sigmoid(y) * b).astype(o_ref.dtype)


def _ffn_in(hn, w_ffn_in, conv_w, conv_b, tm=1024, tn=256):
    s, d = hn.shape
    nj = D_FF // tn
    return pl.pallas_call(
        _ffn_in_body,
        out_shape=jax.ShapeDtypeStruct((s, D_FF), jnp.bfloat16),
        grid=(s // tm, nj),
        in_specs=[
            pl.BlockSpec((tm, d), lambda i, j: (i, 0)),
            pl.BlockSpec((d, tn), lambda i, j: (0, j)),
            pl.BlockSpec((d, tn), lambda i, j: (0, nj + j)),
            pl.BlockSpec((3, tn), lambda i, j: (0, j)),
            pl.BlockSpec((1, tn), lambda i, j: (0, j)),
        ],
        out_specs=pl.BlockSpec((tm, tn), lambda i, j: (i, j)),
        scratch_shapes=[pltpu.VMEM((8, D_FF), jnp.float32)],
        compiler_params=_cparams(2), name="ffn_in",
    )(hn, w_ffn_in, w_ffn_in, conv_w, conv_b.reshape(1, D_FF))


def _ffn_out_body(x_ref, a_ref, w_ref, o_ref):
    o_ref[...] = x_ref[...] + _dot(a_ref[...], w_ref[...])


def _ffn_out(h, act, w, tm=512, tn=256):
    s, d = h.shape
    return pl.pallas_call(
        _ffn_out_body,
        out_shape=jax.ShapeDtypeStruct((s, d), jnp.float32),
        grid=(s // tm, d // tn),
        in_specs=[
            pl.BlockSpec((tm, tn), lambda i, j: (i, j)),
            pl.BlockSpec((tm, D_FF), lambda i, j: (i, 0)),
            pl.BlockSpec((D_FF, tn), lambda i, j: (0, j)),
        ],
        out_specs=pl.BlockSpec((tm, tn), lambda i, j: (i, j)),
        compiler_params=_cparams(2), name="ffn_out",
    )(h, act, w)


def _ple_body(h_ref, hg_ref, wg_ref, p_ref, wp_ref, nw_ref, o_ref, e_ref):
    j = pl.program_id(1)
    tn = o_ref.shape[1]

    @pl.when(j == 0)
    def _():
        e = _dot(p_ref[...].astype(jnp.bfloat16), wp_ref[...])
        ms = jnp.mean(e * e, axis=-1, keepdims=True)
        e_ref[...] = e * lax.rsqrt(ms + EPS) * nw_ref[...]

    gate = _sigmoid(_dot(hg_ref[...], wg_ref[...]))
    o_ref[...] = h_ref[...] + e_ref[:, pl.ds(pl.multiple_of(j * tn, tn), tn)] * gate


def _ple(h, hg, w_gate, p, w_proj, ple_norm_w, tm=512, tn=512):
    s, d = h.shape
    return pl.pallas_call(
        _ple_body,
        out_shape=jax.ShapeDtypeStruct((s, d), jnp.float32),
        grid=(s // tm, d // tn),
        in_specs=[
            pl.BlockSpec((tm, tn), lambda i, j: (i, j)),
            pl.BlockSpec((tm, d), lambda i, j: (i, 0)),
            pl.BlockSpec((d, tn), lambda i, j: (0, j)),
            pl.BlockSpec((tm, PLE_DIM), lambda i, j: (i, 0)),
            pl.BlockSpec((PLE_DIM, d), lambda i, j: (0, 0)),
            pl.BlockSpec((1, d), lambda i, j: (0, 0)),
        ],
        out_specs=pl.BlockSpec((tm, tn), lambda i, j: (i, j)),
        scratch_shapes=[pltpu.VMEM((tm, d), jnp.float32)],
        compiler_params=_cparams(2), name="ple_gate",
    )(h, hg, w_gate, p, w_proj, ple_norm_w.reshape(1, d))


def _layer(h, p, positions, norm1_w, w_in, w_pool, pool_scale, q_norm_w, k_norm_cmp_w, k_norm_slc_w,
           k_norm_win_w, cmp_pos_k, cmp_pos_v, cmp_k_w1, cmp_k_w2, cmp_v_w1, cmp_v_w2, w_o, norm2_w,
           w_ffn_in, conv_w, conv_b, w_ffn_out, w_ple_proj, ple_norm_w, ple_gate_norm_w, w_ple_gate):
    s = h.shape[0]
    bf = jnp.bfloat16
    off_q, off_kv, off_g = POOL_WIDTH, POOL_WIDTH + QKV_W, POOL_WIDTH + 2 * QKV_W
    w_main = jnp.concatenate([w_in[:, off_q:off_kv], w_in[:, off_kv:off_g], w_in[:, :off_q]], axis=1).astype(bf)
    w_gate_cols = jnp.pad(w_in[:, off_g:], ((0, 0), (0, GATE_PAD - N_GATE))).astype(bf)
    inv = ROPE_THETA ** (-jnp.arange(0, ROT_DIM, 2, dtype=jnp.float32) / ROT_DIM)

    xn = _rmsnorm(h, norm1_w)
    z_main = _matmul(xn, w_main, 1024, 512, "in_proj")
    z_g = _matmul(xn, w_gate_cols, 1024, GATE_PAD, "in_proj_gates")
    qT, ks, kw, vsT, vwT, kvc, gT = _prep(z_main, z_g, positions, inv, q_norm_w, k_norm_slc_w, k_norm_win_w)

    n_cmp = (s - CMP_LEN) // CMP_STRIDE + 1
    pos_cmp = positions[CMP_LEN - 1::CMP_STRIDE][:n_cmp]
    pos_cmp = jnp.pad(pos_cmp, (0, s // CMP_STRIDE - n_cmp)).reshape(s // CMP_STRIDE, 1)
    half = CMP_STRIDE * HEAD_DIM
    pe = jnp.stack([cmp_pos_k.reshape(2, half), cmp_pos_v.reshape(2, half)])
    w1 = jnp.stack([cmp_k_w1, cmp_v_w1]).astype(bf)
    w2 = jnp.stack([cmp_k_w2, cmp_v_w2]).astype(bf)
    kc, vcT = _compress(kvc, pe, w1, w2, k_norm_cmp_w, pos_cmp, inv)

    o_attn = _attention(qT, kc, vcT, ks, vsT, kw, vwT, gT)
    o_pool = _pool(z_main, w_pool.astype(bf), pool_scale)
    h = _oproj(h, o_pool, o_attn, w_o[:POOL_WIDTH].astype(bf), w_o[POOL_WIDTH:].astype(bf))

    hn = _rmsnorm(h, norm2_w)
    act = _ffn_in(hn, w_ffn_in.astype(bf), conv_w, conv_b)
    h = _ffn_out(h, act, w_ffn_out.astype(bf))

    hg = _rmsnorm(h, ple_gate_norm_w)
    return _ple(h, hg, w_ple_gate.astype(bf), p, w_ple_proj.astype(bf), ple_norm_w)


def kernel(x, p, positions, norm1_w, w_in, w_pool, pool_scale, q_norm_w, k_norm_cmp_w, k_norm_slc_w, k_norm_win_w, cmp_pos_k, cmp_pos_v, cmp_k_w1, cmp_k_w2, cmp_v_w1, cmp_v_w2, w_o, norm2_w, w_ffn_in, conv_w, conv_b, w_ffn_out, w_ple_proj, ple_norm_w, ple_gate_norm_w, w_ple_gate):
    b, s, d = x.shape
    assert b == 1 and d == D_MODEL and s % 1024 == 0 and s >= WIN_SPAN
    h = x.reshape(s, d)
    for i in range(p.shape[0]):
        h = _layer(h, p[i, 0], positions[0], norm1_w[i], w_in[i], w_pool[i], pool_scale[i], q_norm_w[i],
                   k_norm_cmp_w[i], k_norm_slc_w[i], k_norm_win_w[i], cmp_pos_k[i], cmp_pos_v[i],
                   cmp_k_w1[i], cmp_k_w2[i], cmp_v_w1[i], cmp_v_w2[i], w_o[i], norm2_w[i], w_ffn_in[i],
                   conv_w[i], conv_b[i], w_ffn_out[i], w_ple_proj[i], ple_norm_w[i], ple_gate_norm_w[i],
                   w_ple_gate[i])
    return h.reshape(b, s, d)
```
